```python
import math
import jax
import jax.numpy as jnp
from jax import lax
import numpy as np

D_MODEL = 1024
BATCH = 4
SEQ = 4096
DEPTH = 1
DEC_BATCH = 128
DEC_SEQ = 1
PAST_LEN = 2048
PAGE_SIZE = 128

HEAD_DIM = 64
SB_HEADS = 8
NSA_HEADS = 8
NSA_KV_HEADS = 2
NSA_REP = NSA_HEADS // NSA_KV_HEADS
SB_WIDTH = SB_HEADS * HEAD_DIM
NSA_WIDTH = NSA_HEADS * HEAD_DIM
MIX_WIDTH = SB_WIDTH + NSA_WIDTH
KV_WIDTH = NSA_KV_HEADS * HEAD_DIM
N_BRANCH = 3
IN_COLS = 3 * SB_WIDTH + NSA_WIDTH + 6 * KV_WIDTH + N_BRANCH * NSA_HEADS
CMP_LEN = 32
CMP_STRIDE = 16
CMP_HIDDEN = 2 * HEAD_DIM
SEL_LEN = 64
SEL_TOPK = 16
FORCE_SCORE = 1.0e4
WINDOW = 512
ROPE_THETA = 10000.0
N_GROUPS = 4
EXPERTS_PER_GROUP = 4
N_EXPERTS = N_GROUPS * EXPERTS_PER_GROUP
TOP_K_EXPERTS = 2
D_FF_EXPERT = 512
Q_BLOCK = 128
NORM_EPS = 1e-6
NEG_BIG = -1.0e30

kernel_name = 'hymba_stickbreak_nsa_hiermoe_step'


def rms_norm(x, g):
    xf = x.astype(jnp.float32)
    y = xf * lax.rsqrt(jnp.mean(xf * xf, axis=-1, keepdims=True) + NORM_EPS)
    return (y * g.astype(jnp.float32)).astype(x.dtype)


def rotary(x, pos):
    half = HEAD_DIM // 2
    inv_freq = ROPE_THETA ** (-jnp.arange(half, dtype=jnp.float32) / half)
    ang = pos.astype(jnp.float32)[:, None] * inv_freq[None, :]
    bshape = (1, pos.shape[0]) + (1,) * (x.ndim - 3) + (half,)
    cos = jnp.cos(ang).reshape(bshape)
    sin = jnp.sin(ang).reshape(bshape)
    xf = x.astype(jnp.float32)
    x1, x2 = xf[..., :half], xf[..., half:]
    return jnp.concatenate([x1 * cos - x2 * sin, x2 * cos + x1 * sin], axis=-1).astype(x.dtype)


def project(h, w_in, pos):
    B, L = h.shape[:2]
    sizes = [SB_WIDTH] * 3 + [NSA_WIDTH] + [KV_WIDTH] * 6 + [N_BRANCH * NSA_HEADS]
    cuts = np.cumsum(sizes)[:-1].tolist()
    parts = jnp.split(jnp.einsum('bld,dc->blc', h, w_in), cuts, axis=-1)
    sb_q, sb_k, sb_v = [t.reshape(B, L, SB_HEADS, HEAD_DIM) for t in parts[:3]]
    q = rotary(parts[3].reshape(B, L, NSA_KV_HEADS, NSA_REP, HEAD_DIM), pos)
    ck, cv, sk, sv, wk, wv = [t.reshape(B, L, NSA_KV_HEADS, HEAD_DIM) for t in parts[4:10]]
    ck, sk, wk = rotary(ck, pos), rotary(sk, pos), rotary(wk, pos)
    gates = jax.nn.sigmoid(parts[10].astype(jnp.float32)).reshape(B, L, NSA_KV_HEADS, NSA_REP, N_BRANCH)
    return sb_q, sb_k, sb_v, q, ck, cv, sk, sv, wk, wv, gates


def sweep_query_blocks(fn, q, q_pos):
    B, T = q.shape[:2]
    nb = T // Q_BLOCK
    qb = jnp.moveaxis(q.reshape((B, nb, Q_BLOCK) + q.shape[2:]), 1, 0)
    pb = q_pos.reshape(nb, Q_BLOCK)
    out = lax.map(lambda args: fn(*args), (qb, pb))
    return jax.tree_util.tree_map(lambda o: jnp.moveaxis(o, 0, 1).reshape((B, T) + o.shape[3:]), out)


def stick_breaking_attend(q, k, v, q_pos, k_pos):
    z = jnp.einsum('bqhd,blhd->bhql', q, k).astype(jnp.float32) / math.sqrt(HEAD_DIM)
    mask = k_pos[None, :] < q_pos[:, None]
    log_keep = jnp.where(mask, jax.nn.log_sigmoid(-z), 0.0)
    log_rest = lax.cumsum(log_keep, axis=3, reverse=True) - log_keep
    a = jnp.where(mask, jnp.exp(jax.nn.log_sigmoid(z) + log_rest), 0.0)
    return jnp.einsum('bhql,blhd->bqhd', a.astype(v.dtype), v)


def compressed_block_ends(length):
    nc = (length - CMP_LEN) // CMP_STRIDE + 1
    return jnp.arange(nc, dtype=jnp.int32) * CMP_STRIDE + CMP_LEN - 1


def compress_blocks(k, pe, w1, w2):
    B, L = k.shape[:2]
    nc = (L - CMP_LEN) // CMP_STRIDE + 1
    ratio = CMP_LEN // CMP_STRIDE
    chunks = k[:, :(nc + ratio - 1) * CMP_STRIDE].reshape(B, nc + ratio - 1, CMP_STRIDE, NSA_KV_HEADS, HEAD_DIM)
    blocks = jnp.concatenate([chunks[:, r:r + nc] for r in range(ratio)], axis=2)
    blocks = blocks + pe[None, None, :, None, :].astype(k.dtype)
    flat = jnp.moveaxis(blocks, 3, 2).reshape(B, nc, NSA_KV_HEADS, CMP_LEN * HEAD_DIM)
    hid = jax.nn.gelu(jnp.einsum('bnge,eh->bngh', flat, w1))
    return jnp.einsum('bngh,hd->bngd', hid, w2)


def nsa_compressed_selected(q, q_pos, kc, vc, c_end, k_sel, v_sel):
    B, Q = q.shape[:2]
    L = k_sel.shape[1]
    scale = 1.0 / math.sqrt(HEAD_DIM)
    s = jnp.einsum('bqgrd,bngd->bgrqn', q, kc).astype(jnp.float32) * scale
    cmask = c_end[None, :] <= q_pos[:, None]
    s = jnp.where(cmask, s, NEG_BIG)
    m = jnp.max(s, axis=-1, keepdims=True)
    e = jnp.where(cmask, jnp.exp(s - m), 0.0)
    p = e / jnp.maximum(jnp.sum(e, axis=-1, keepdims=True), 1e-30)
    o_cmp = jnp.einsum('bgrqn,bngd->bqgrd', p.astype(vc.dtype), vc)
    n_sel = -(-L // SEL_LEN)
    j_start = jnp.arange(n_sel, dtype=jnp.int32) * SEL_LEN
    c_start = c_end - (CMP_LEN - 1)
    cover = (c_start[:, None] < j_start[None, :] + SEL_LEN) & (c_end[:, None] >= j_start[None, :])
    imp = jnp.einsum('bgrqn,nj->bgqj', p, cover.astype(jnp.float32))
    jj = jnp.arange(n_sel, dtype=jnp.int32)[None, :]
    q_blk = (q_pos // SEL_LEN)[:, None]
    forced = (jj == 0) | (jj == q_blk) | (jj == q_blk - 1)
    valid = j_start[None, :] <= q_pos[:, None]
    imp = jnp.where(forced, FORCE_SCORE, jnp.where(valid, imp, -1.0))
    _, idx = lax.top_k(imp, min(SEL_TOPK, n_sel))
    pos = (idx[..., None] * SEL_LEN + jnp.arange(SEL_LEN, dtype=jnp.int32)).reshape(B, NSA_KV_HEADS, Q, -1)
    pos_c = jnp.minimum(pos, L - 1)
    gather = jax.vmap(jax.vmap(lambda a, i: a[i]))
    kg = gather(jnp.moveaxis(k_sel, 2, 1), pos_c)
    vg = gather(jnp.moveaxis(v_sel, 2, 1), pos_c)
    s2 = jnp.einsum('bqgrd,bgqkd->bgrqk', q, kg).astype(jnp.float32) * scale
    smask = (pos <= q_pos[None, None, :, None])[:, :, None]
    p2 = jax.nn.softmax(jnp.where(smask, s2, NEG_BIG), axis=-1)
    o_sel = jnp.einsum('bgrqk,bgqkd->bqgrd', p2.astype(vg.dtype), vg)
    return o_cmp, o_sel


def window_attend(q, q_pos, k, v, k_pos):
    s = jnp.einsum('bqgrd,blgd->bgrql', q, k).astype(jnp.float32) / math.sqrt(HEAD_DIM)
    kp, qp = k_pos[None, :], q_pos[:, None]
    mask = (kp <= qp) & (kp >= qp - WINDOW) & (kp >= 0)
    p = jax.nn.softmax(jnp.where(mask, s, NEG_BIG), axis=-1)
    return jnp.einsum('bgrql,blgd->bqgrd', p.astype(v.dtype), v)


def merge_nsa_branches(o_cmp, o_sel, o_win, gates):
    B, L = o_cmp.shape[:2]
    o = gates[..., 0:1] * o_cmp + gates[..., 1:2] * o_sel + gates[..., 2:3] * o_win
    return o.astype(o_cmp.dtype).reshape(B, L, NSA_WIDTH)


def gather_pages(cache, page_table):
    g = cache[page_table]
    return g.reshape((g.shape[0], -1) + cache.shape[2:])


def mixer_prompt(h, w_in, cmp_pe_k, cmp_pe_v, cmp_k_w1, cmp_k_w2, cmp_v_w1, cmp_v_w2):
    B, T = h.shape[:2]
    pos = jnp.arange(T, dtype=jnp.int32)
    sb_q, sb_k, sb_v, q, ck, cv, sk, sv, wk, wv, gates = project(h, w_in, pos)
    o_sb = sweep_query_blocks(lambda qb, pb: stick_breaking_attend(qb, sb_k, sb_v, pb, pos), sb_q, pos)
    kc = compress_blocks(ck, cmp_pe_k, cmp_k_w1, cmp_k_w2)
    vc = compress_blocks(cv, cmp_pe_v, cmp_v_w1, cmp_v_w2)
    c_end = compressed_block_ends(T)
    o_cmp, o_sel = sweep_query_blocks(
        lambda qb, pb: nsa_compressed_selected(qb, pb, kc, vc, c_end, sk, sv), q, pos)
    wk_pad = jnp.pad(wk, ((0, 0), (WINDOW, 0), (0, 0), (0, 0)))
    wv_pad = jnp.pad(wv, ((0, 0), (WINDOW, 0), (0, 0), (0, 0)))

    def win_block(qb, pb):
        start = pb[0]
        kb = lax.dynamic_slice_in_dim(wk_pad, start, WINDOW + Q_BLOCK, axis=1)
        vb = lax.dynamic_slice_in_dim(wv_pad, start, WINDOW + Q_BLOCK, axis=1)
        kpos = start - WINDOW + jnp.arange(WINDOW + Q_BLOCK, dtype=jnp.int32)
        return window_attend(qb, pb, kb, vb, kpos)

    o_win = sweep_query_blocks(win_block, q, pos)
    o_nsa = merge_nsa_branches(o_cmp, o_sel, o_win, gates)
    keep = min(WINDOW, T)
    new_state = (sb_k, sb_v, ck, cv, sk, sv, wk[:, T - keep:], wv[:, T - keep:])
    return o_sb.reshape(B, T, SB_WIDTH), o_nsa, new_state


def mixer_sample(h, cache_sb_k, cache_sb_v, cache_cmp_k, cache_cmp_v, cache_sel_k, cache_sel_v,
                 state_win_k, state_win_v, page_table, w_in, cmp_pe_k, cmp_pe_v,
                 cmp_k_w1, cmp_k_w2, cmp_v_w1, cmp_v_w2):
    Bd, S = h.shape[:2]
    past = page_table.shape[1] * cache_sb_k.shape[1]
    q_pos = past + jnp.arange(S, dtype=jnp.int32)
    all_pos = jnp.arange(past + S, dtype=jnp.int32)
    sb_q, sb_k, sb_v, q, ck, cv, sk, sv, wk, wv, gates = project(h, w_in, q_pos)

    def with_past(cache, new):
        return jnp.concatenate([gather_pages(cache, page_table).astype(new.dtype), new], axis=1)

    o_sb = stick_breaking_attend(sb_q, with_past(cache_sb_k, sb_k), with_past(cache_sb_v, sb_v), q_pos, all_pos)
    kc = compress_blocks(with_past(cache_cmp_k, ck), cmp_pe_k, cmp_k_w1, cmp_k_w2)
    vc = compress_blocks(with_past(cache_cmp_v, cv), cmp_pe_v, cmp_v_w1, cmp_v_w2)
    c_end = compressed_block_ends(past + S)
    o_cmp, o_sel = nsa_compressed_selected(q, q_pos, kc, vc, c_end,
                                           with_past(cache_sel_k, sk), with_past(cache_sel_v, sv))
    keep = state_win_k.shape[1]
    wk_all = jnp.concatenate([state_win_k.astype(wk.dtype), wk], axis=1)
    wv_all = jnp.concatenate([state_win_v.astype(wv.dtype), wv], axis=1)
    wpos = past - keep + jnp.arange(keep + S, dtype=jnp.int32)
    o_win = window_attend(q, q_pos, wk_all, wv_all, wpos)
    o_nsa = merge_nsa_branches(o_cmp, o_sel, o_win, gates)
    new_state = (sb_k, sb_v, ck, cv, sk, sv, wk_all[:, S:], wv_all[:, S:])
    return o_sb.reshape(Bd, S, SB_WIDTH), o_nsa, new_state


def hierarchical_moe(h, router_group_w, router_group_b, router_expert_w, router_expert_b, w_gate, w_up, w_down):
    g_logits = jnp.einsum('nd,dg->ng', h, router_group_w).astype(jnp.float32) + router_group_b.astype(jnp.float32)
    g_prob = jax.nn.softmax(g_logits, axis=-1)
    g_idx = jnp.argmax(g_logits, axis=-1)
    g_w = jnp.take_along_axis(g_prob, g_idx[:, None], axis=-1)
    e_logits = jnp.einsum('nd,dge->nge', h, router_expert_w).astype(jnp.float32) + router_expert_b.astype(jnp.float32)
    e_logits = jnp.take_along_axis(e_logits, g_idx[:, None, None], axis=1)[:, 0]
    top_w, top_i = lax.top_k(jax.nn.softmax(e_logits, axis=-1), TOP_K_EXPERTS)
    top_w = top_w / jnp.sum(top_w, axis=-1, keepdims=True)
    expert_id = g_idx[:, None] * EXPERTS_PER_GROUP + top_i
    combine = jnp.sum(jax.nn.one_hot(expert_id, N_EXPERTS, dtype=jnp.float32) * (g_w * top_w)[..., None], axis=1)
    gate = jnp.einsum('nd,edf->nef', h, w_gate)
    up = jnp.einsum('nd,edf->nef', h, w_up)
    act = jax.nn.silu(gate) * up * combine[..., None].astype(h.dtype)
    return jnp.einsum('nef,efd->nd', act, w_down)


def finish_layer(x, o_sb, o_nsa, norm_sb_out, norm_nsa_out, w_out, norm_ffn, router_group_w, router_group_b,
                 router_expert_w, router_expert_b, w_gate, w_up, w_down):
    mix = jnp.concatenate([rms_norm(o_sb, norm_sb_out), rms_norm(o_nsa, norm_nsa_out)], axis=-1)
    x = x + jnp.einsum('blc,cd->bld', mix, w_out)
    B, L, D = x.shape
    h = rms_norm(x, norm_ffn).reshape(B * L, D)
    y = hierarchical_moe(h, router_group_w, router_group_b, router_expert_w, router_expert_b, w_gate, w_up, w_down)
    return x + y.reshape(B, L, D)


def setup_inputs(seed: int = 0) -> dict:
    key = jax.random.key(seed)
    ks = jax.random.split(key, 32)
    f32 = jnp.float32
    n_pages = PAST_LEN // PAGE_SIZE
    n_used = DEC_BATCH * n_pages
    n_pool = (n_used * 5) // 4
    win_keep = min(WINDOW, PAST_LEN)

    def normal(k, shape, scale=1.0):
        return scale * jax.random.normal(k, shape, f32)

    def gain(k, shape):
        return 1.0 + 0.02 * jax.random.normal(k, shape, f32)

    page_table = jax.random.permutation(ks[10], n_pool)[:n_used].reshape(DEC_BATCH, n_pages).astype(jnp.int32)
    return {
        'x_prompt': normal(ks[0], (BATCH, SEQ, D_MODEL)),
        'x_sample': normal(ks[1], (DEC_BATCH, DEC_SEQ, D_MODEL)),
        'cache_sb_k': normal(ks[2], (DEPTH, n_pool, PAGE_SIZE, SB_HEADS, HEAD_DIM)),
        'cache_sb_v': normal(ks[3], (DEPTH, n_pool, PAGE_SIZE, SB_HEADS, HEAD_DIM)),
        'cache_cmp_k': normal(ks[4], (DEPTH, n_pool, PAGE_SIZE, NSA_KV_HEADS, HEAD_DIM)),
        'cache_cmp_v': normal(ks[5], (DEPTH, n_pool, PAGE_SIZE, NSA_KV_HEADS, HEAD_DIM)),
        'cache_sel_k': normal(ks[6], (DEPTH, n_pool, PAGE_SIZE, NSA_KV_HEADS, HEAD_DIM)),
        'cache_sel_v': normal(ks[7], (DEPTH, n_pool, PAGE_SIZE, NSA_KV_HEADS, HEAD_DIM)),
        'state_win_k': normal(ks[8], (DEPTH, DEC_BATCH, win_keep, NSA_KV_HEADS, HEAD_DIM)),
        'state_win_v': normal(ks[9], (DEPTH, DEC_BATCH, win_keep, NSA_KV_HEADS, HEAD_DIM)),
        'page_table': page_table,
        'w_in': normal(ks[11], (DEPTH, D_MODEL, IN_COLS), D_MODEL ** -0.5),
        'cmp_pe_k': normal(ks[12], (DEPTH, CMP_LEN, HEAD_DIM), 0.1),
        'cmp_pe_v': normal(ks[13], (DEPTH, CMP_LEN, HEAD_DIM), 0.1),
        'cmp_k_w1': normal(ks[14], (DEPTH, CMP_LEN * HEAD_DIM, CMP_HIDDEN), (CMP_LEN * HEAD_DIM) ** -0.5),
        'cmp_k_w2': normal(ks[15], (DEPTH, CMP_HIDDEN, HEAD_DIM), CMP_HIDDEN ** -0.5),
        'cmp_v_w1': normal(ks[16], (DEPTH, CMP_LEN * HEAD_DIM, CMP_HIDDEN), (CMP_LEN * HEAD_DIM) ** -0.5),
        'cmp_v_w2': normal(ks[17], (DEPTH, CMP_HIDDEN, HEAD_DIM), CMP_HIDDEN ** -0.5),
        'norm_sb_out': gain(ks[18], (DEPTH, SB_WIDTH)),
        'norm_nsa_out': gain(ks[19], (DEPTH, NSA_WIDTH)),
        'w_out': normal(ks[20], (DEPTH, MIX_WIDTH, D_MODEL), MIX_WIDTH ** -0.5),
        'norm_attn': gain(ks[21], (DEPTH, D_MODEL)),
        'norm_ffn': gain(ks[22], (DEPTH, D_MODEL)),
        'router_group_w': normal(ks[23], (DEPTH, D_MODEL, N_GROUPS), D_MODEL ** -0.5),
        'router_group_b': normal(ks[24], (DEPTH, N_GROUPS), 0.01),
        'router_expert_w': normal(ks[25], (DEPTH, D_MODEL, N_GROUPS, EXPERTS_PER_GROUP), D_MODEL ** -0.5),
        'router_expert_b': normal(ks[26], (DEPTH, N_GROUPS, EXPERTS_PER_GROUP), 0.01),
        'w_gate': normal(ks[27], (DEPTH, N_EXPERTS, D_MODEL, D_FF_EXPERT), D_MODEL ** -0.5),
        'w_up': normal(ks[28], (DEPTH, N_EXPERTS, D_MODEL, D_FF_EXPERT), D_MODEL ** -0.5),
        'w_down': normal(ks[29], (DEPTH, N_EXPERTS, D_FF_EXPERT, D_MODEL), D_FF_EXPERT ** -0.5),
        'norm_final': gain(ks[30], (D_MODEL,)),
    }


def reference(x_prompt, x_sample, cache_sb_k, cache_sb_v, cache_cmp_k, cache_cmp_v, cache_sel_k, cache_sel_v,
              state_win_k, state_win_v, page_table, w_in, cmp_pe_k, cmp_pe_v, cmp_k_w1, cmp_k_w2,
              cmp_v_w1, cmp_v_w2, norm_sb_out, norm_nsa_out, w_out, norm_attn, norm_ffn,
              router_group_w, router_group_b, router_expert_w, router_expert_b, w_gate, w_up, w_down,
              norm_final):
    xp, xs = x_prompt, x_sample
    layer_states = []
    for l in range(DEPTH):
        hp = rms_norm(xp, norm_attn[l])
        o_sb_p, o_nsa_p, st_p = mixer_prompt(hp, w_in[l], cmp_pe_k[l], cmp_pe_v[l],
                                             cmp_k_w1[l], cmp_k_w2[l], cmp_v_w1[l], cmp_v_w2[l])
        xp = finish_layer(xp, o_sb_p, o_nsa_p, norm_sb_out[l], norm_nsa_out[l], w_out[l], norm_ffn[l],
                          router_group_w[l], router_group_b[l], router_expert_w[l], router_expert_b[l],
                          w_gate[l], w_up[l], w_down[l])
        hs = rms_norm(xs, norm_attn[l])
        o_sb_s, o_nsa_s, st_s = mixer_sample(hs, cache_sb_k[l], cache_sb_v[l], cache_cmp_k[l], cache_cmp_v[l],
                                             cache_sel_k[l], cache_sel_v[l], state_win_k[l], state_win_v[l],
                                             page_table, w_in[l], cmp_pe_k[l], cmp_pe_v[l],
                                             cmp_k_w1[l], cmp_k_w2[l], cmp_v_w1[l], cmp_v_w2[l])
        xs = finish_layer(xs, o_sb_s, o_nsa_s, norm_sb_out[l], norm_nsa_out[l], w_out[l], norm_ffn[l],
                          router_group_w[l], router_group_b[l], router_expert_w[l], router_expert_b[l],
                          w_gate[l], w_up[l], w_down[l])
        layer_states.append(st_p + st_s)
    (sb_k_p, sb_v_p, cmp_k_p, cmp_v_p, sel_k_p, sel_v_p, win_k_p, win_v_p,
     sb_k_s, sb_v_s, cmp_k_s, cmp_v_s, sel_k_s, sel_v_s, win_k_s, win_v_s) = [jnp.stack(t) for t in zip(*layer_states)]
    y_prompt = rms_norm(xp, norm_final)
    y_sample = rms_norm(xs, norm_final)
    return (y_prompt, y_sample, sb_k_p, sb_v_p, cmp_k_p, cmp_v_p, sel_k_p, sel_v_p, win_k_p, win_v_p,
            sb_k_s, sb_v_s, cmp_k_s, cmp_v_s, sel_k_s, sel_v_s, win_k_s, win_v_s)
```

```python
import functools
import math

import jax
import jax.numpy as jnp
from jax import lax
from jax.experimental import pallas as pl
from jax.experimental.pallas import tpu as pltpu

F32 = jnp.float32
BF16 = jnp.bfloat16

D_MODEL = 1024
HEAD_DIM = 64
SB_HEADS = 8
NSA_HEADS = 8
NSA_KV_HEADS = 2
NSA_REP = NSA_HEADS // NSA_KV_HEADS
SB_WIDTH = SB_HEADS * HEAD_DIM
NSA_WIDTH = NSA_HEADS * HEAD_DIM
KV_WIDTH = NSA_KV_HEADS * HEAD_DIM
N_BRANCH = 3
N_GATES = N_BRANCH * NSA_HEADS
CMP_LEN = 32
CMP_STRIDE = 16
CMP_HIDDEN = 2 * HEAD_DIM
SEL_LEN = 64
SEL_TOPK = 16
FORCE_SCORE = 1.0e4
WINDOW = 512
ROPE_THETA = 10000.0
N_GROUPS = 4
EXPERTS_PER_GROUP = 4
N_EXPERTS = N_GROUPS * EXPERTS_PER_GROUP
D_FF_EXPERT = 512
NORM_EPS = 1e-6
NEG_BIG = -1.0e30
PAGE_SIZE = 128
Q_SCALE = 1.0 / math.sqrt(HEAD_DIM)

LANES = 128
C_SBQ, C_SBK, C_SBV, C_NQ = 0, 512, 1024, 1536
C_CK, C_CV, C_SK, C_SV, C_WK, C_WV, C_GATE = 2048, 2176, 2304, 2432, 2560, 2688, 2816
IN_COLS_PAD = 2944
CHUNK_FLAT = CMP_STRIDE * KV_WIDTH
ROUTER_ROWS = 32
V7X_VMEM_LIMIT = 56 * 1024 * 1024


def _cparams(sem, vmem=V7X_VMEM_LIMIT):
    return pltpu.CompilerParams(dimension_semantics=sem, vmem_limit_bytes=vmem)


def _nt(a, b):
    return lax.dot_general(a, b, (((1,), (1,)), ((), ())), preferred_element_type=F32)


def _nn(a, b):
    return jnp.dot(a, b, preferred_element_type=F32)


def _iota(shape, dim):
    return lax.broadcasted_iota(jnp.int32, shape, dim)


def _rms(x, g):
    return x * lax.rsqrt(jnp.mean(x * x, axis=-1, keepdims=True) + NORM_EPS) * g


def _softplus(s):
    return jnp.maximum(s, 0.0) + jnp.log(1.0 + jnp.exp(-jnp.abs(s)))


def _split_bf16(x):
    hi = x.astype(BF16)
    lo = (x - hi.astype(F32)).astype(BF16)
    return hi, lo


def _swap_halves(x):
    return pltpu.roll(x, 64, 1)


def _proj_kernel(x_ref, g_ref, w_ref, cos_ref, sin_ref,
                 sbk_ref, sbv_ref, ck_ref, cv_ref, sk_ref, sv_ref, wk_ref, wv_ref, gate_ref,
                 sbq2_ref, sbkb_ref, sbvb_ref, qn_ref, skb_ref, svb_ref, wkb_ref, wvb_ref):
    h = _rms(x_ref[...], g_ref[...]).astype(BF16)
    tm = h.shape[0]
    lane = _iota((tm, LANES), 1)
    low_half = lane < 64
    first_quarter = (lane & 63) < 32
    cos = cos_ref[...]
    sin = sin_ref[...]

    def mm(c0, width):
        return _nn(h, w_ref[:, c0:c0 + width])

    def rope(xc):
        rot = jnp.where(first_quarter, pltpu.roll(xc, 96, 1), pltpu.roll(xc, 32, 1))
        return xc * cos + rot * sin

    q = mm(C_SBQ, SB_WIDTH) * Q_SCALE
    for c in range(SB_WIDTH // LANES):
        qc = q[:, c * LANES:(c + 1) * LANES]
        sbq2_ref[:, (2 * c) * LANES:(2 * c + 1) * LANES] = jnp.where(low_half, qc, 0.0).astype(BF16)
        sbq2_ref[:, (2 * c + 1) * LANES:(2 * c + 2) * LANES] = jnp.where(low_half, 0.0, qc).astype(BF16)
    k = mm(C_SBK, SB_WIDTH)
    sbk_ref[...] = k
    sbkb_ref[...] = k.astype(BF16)
    v = mm(C_SBV, SB_WIDTH)
    sbv_ref[...] = v
    sbvb_ref[...] = v.astype(BF16)

    q = mm(C_NQ, NSA_WIDTH)
    for c in range(NSA_WIDTH // LANES):
        qc = rope(q[:, c * LANES:(c + 1) * LANES]) * Q_SCALE
        qs = _swap_halves(qc)
        for half in range(2):
            hd = 2 * c + half
            grp = hd // NSA_REP
            src = qc if half == grp else qs
            keep = low_half if grp == 0 else jnp.logical_not(low_half)
            qn_ref[:, hd * LANES:(hd + 1) * LANES] = jnp.where(keep, src, 0.0).astype(BF16)

    ck_ref[...] = rope(mm(C_CK, KV_WIDTH))
    cv_ref[...] = mm(C_CV, KV_WIDTH)
    sk = rope(mm(C_SK, KV_WIDTH))
    sk_ref[...] = sk
    skb_ref[...] = sk.astype(BF16)
    sv = mm(C_SV, KV_WIDTH)
    sv_ref[...] = sv
    svb_ref[...] = sv.astype(BF16)
    wk = rope(mm(C_WK, KV_WIDTH))
    wk_ref[...] = wk
    wkb_ref[...] = wk.astype(BF16)
    wv = mm(C_WV, KV_WIDTH)
    wv_ref[...] = wv
    wvb_ref[...] = wv.astype(BF16)
    gate_ref[...] = 1.0 / (1.0 + jnp.exp(-mm(C_GATE, LANES)))


def _proj(x, gain, w, cos, sin, tm, table_blocks):
    n = x.shape[0]
    row = lambda width: pl.BlockSpec((tm, width), lambda i: (i, 0))
    table = pl.BlockSpec((tm, LANES), lambda i: (i % table_blocks, 0))
    f32_widths = [SB_WIDTH, SB_WIDTH] + [KV_WIDTH] * 6 + [LANES]
    bf_widths = [2 * SB_WIDTH, SB_WIDTH, SB_WIDTH, NSA_HEADS * LANES] + [KV_WIDTH] * 4
    return pl.pallas_call(
        _proj_kernel,
        grid=(n // tm,),
        in_specs=[row(D_MODEL), pl.BlockSpec((1, D_MODEL), lambda i: (0, 0)),
                  pl.BlockSpec((D_MODEL, IN_COLS_PAD), lambda i: (0, 0)), table, table],
        out_specs=[row(wd) for wd in f32_widths + bf_widths],
        out_shape=[jax.ShapeDtypeStruct((n, wd), F32) for wd in f32_widths]
        + [jax.ShapeDtypeStruct((n, wd), BF16) for wd in bf_widths],
        compiler_params=_cparams(("parallel",)),
        name="proj",
    )(x, gain, w, cos, sin)


SB_TQ = 256


def _sb_tile(q2, kj, vj, upper, carry, diag):
    s = _nt(q2, kj)
    sp = _softplus(s)
    log_keep = -sp
    if diag:
        tq = q2.shape[0] // 2
        row = _iota(s.shape, 0) & (tq - 1)
        col = _iota(s.shape, 1)
        mask = col < row
        log_keep = jnp.where(mask, log_keep, 0.0)
    hi, lo = _split_bf16(log_keep)
    rest = _nn(hi, upper) + _nn(lo, upper) + carry
    a = jnp.exp((s - sp) + rest)
    if diag:
        a = jnp.where(mask, a, 0.0)
    return _nn(a.astype(BF16), vj), jnp.sum(log_keep, axis=-1, keepdims=True)


def _sb_prompt_kernel(q_ref, k_ref, v_ref, u_ref, o_ref, acc_ref, carry_ref):
    qi = pl.program_id(2)
    tq = SB_TQ
    q2 = jnp.concatenate([q_ref[0, :, :LANES], q_ref[0, :, LANES:]], axis=0)
    upper = u_ref[...]

    def tile(j, diag):
        start = pl.multiple_of(j * tq, tq)
        return _sb_tile(q2, k_ref[0, pl.ds(start, tq), :], v_ref[0, pl.ds(start, tq), :],
                        upper, carry_ref[...], diag)

    carry_ref[...] = jnp.zeros_like(carry_ref)
    o, ksum = tile(qi, True)
    acc_ref[...] = o
    carry_ref[...] = ksum

    def body(i, _):
        o, ksum = tile(qi - 1 - i, False)
        acc_ref[...] += o
        carry_ref[...] += ksum
        return 0

    lax.fori_loop(0, qi, body, 0)
    acc = acc_ref[...]
    o_ref[0] = jnp.where(_iota((tq, LANES), 1) < 64, acc[:tq], acc[tq:])


def _strict_upper(n):
    return (jnp.arange(n)[:, None] > jnp.arange(n)[None, :]).astype(BF16)


def _sb_prompt(q2, kb, vb):
    b, t, _ = kb.shape
    pairs = SB_WIDTH // LANES
    tq = SB_TQ
    return pl.pallas_call(
        _sb_prompt_kernel,
        grid=(b, pairs, t // tq),
        in_specs=[pl.BlockSpec((1, tq, 2 * LANES), lambda bi, p, qi: (bi, qi, p)),
                  pl.BlockSpec((1, t, LANES), lambda bi, p, qi: (bi, 0, p)),
                  pl.BlockSpec((1, t, LANES), lambda bi, p, qi: (bi, 0, p)),
                  pl.BlockSpec((tq, tq), lambda bi, p, qi: (0, 0))],
        out_specs=pl.BlockSpec((1, tq, LANES), lambda bi, p, qi: (bi, qi, p)),
        out_shape=jax.ShapeDtypeStruct((b, t, SB_WIDTH), F32),
        scratch_shapes=[pltpu.VMEM((2 * tq, LANES), F32), pltpu.VMEM((2 * tq, 1), F32)],
        compiler_params=_cparams(("parallel", "parallel", "arbitrary")),
        name="sb_prompt",
    )(q2, kb, vb, _strict_upper(tq))


def _compress(c, pea, peb, w1a, w1b, w2, n_valid):
    rows = c.shape[0]
    ya = _nn((c + pea).astype(BF16), w1a)
    yb = _nn((c + peb).astype(BF16), w1b)
    hid = ya + pltpu.roll(yb, rows - 1, 0)
    act = 0.5 * hid * (1.0 + jnp.tanh(math.sqrt(2.0 / math.pi) * (hid + 0.044715 * (hid * hid * hid))))
    out = _nn(act.astype(BF16), w2)
    return jnp.where(_iota(out.shape, 0) < n_valid, out, 0.0).astype(BF16)


def _compress_prompt_kernel(c_ref, pea_ref, peb_ref, w1a_ref, w1b_ref, w2_ref, o_ref, *, n_valid):
    o_ref[0] = _compress(c_ref[0], pea_ref[...], peb_ref[...], w1a_ref[...], w1b_ref[...], w2_ref[...], n_valid)


def _compress_weights(pe, w1, w2):
    eye = jnp.eye(NSA_KV_HEADS, dtype=F32)
    w1r = w1.reshape(CMP_LEN, HEAD_DIM, CMP_HIDDEN)

    def expand(part):
        return jnp.einsum('ldh,pg->lpdgh', part, eye).reshape(CHUNK_FLAT, NSA_KV_HEADS * CMP_HIDDEN).astype(BF16)

    def pe_flat(part):
        return jnp.broadcast_to(part[:, None, :], (CMP_STRIDE, NSA_KV_HEADS, HEAD_DIM)).reshape(1, CHUNK_FLAT)

    w2x = jnp.einsum('hd,pg->phgd', w2, eye).reshape(NSA_KV_HEADS * CMP_HIDDEN, KV_WIDTH).astype(BF16)
    return (pe_flat(pe[:CMP_STRIDE]), pe_flat(pe[CMP_STRIDE:]), expand(w1r[:CMP_STRIDE]), expand(w1r[CMP_STRIDE:]), w2x)


def _const_spec(arr):
    nd = arr.ndim
    return pl.BlockSpec(arr.shape, lambda *_: (0,) * nd)


def _compress_prompt(c, weights, n_valid):
    b, rows, _ = c.shape
    return pl.pallas_call(
        functools.partial(_compress_prompt_kernel, n_valid=n_valid),
        grid=(b,),
        in_specs=[pl.BlockSpec((1, rows, CHUNK_FLAT), lambda i: (i, 0, 0))] + [_const_spec(w) for w in weights],
        out_specs=pl.BlockSpec((1, rows, KV_WIDTH), lambda i: (i, 0, 0)),
        out_shape=jax.ShapeDtypeStruct((b, rows, KV_WIDTH), BF16),
        compiler_params=_cparams(("parallel",)),
        name="compress_prompt",
    )(c, *weights)


def _compress_decode_kernel(pt_ref, *refs, n_pages, n_valid):
    kp, vp = refs[:n_pages], refs[n_pages:2 * n_pages]
    kw, vw = refs[2 * n_pages:2 * n_pages + 5], refs[2 * n_pages + 5:2 * n_pages + 10]
    ko_ref, vo_ref = refs[2 * n_pages + 10:]
    for pages, w, o_ref in ((kp, kw, ko_ref), (vp, vw, vo_ref)):
        c = jnp.concatenate([p[0] for p in pages], axis=0)
        o_ref[0] = _compress(c, *(r[...] for r in w), n_valid)


def _page_specs(block, n_pages):
    return [pl.BlockSpec(block, lambda b, pt, p=p: (pt[b, p], 0, 0)) for p in range(n_pages)]


def _compress_decode(page_table, ck_pages, cv_pages, kweights, vweights, n_valid):
    nb, n_pages = page_table.shape
    rows = n_pages * (PAGE_SIZE // CMP_STRIDE)
    out = pl.BlockSpec((1, rows, KV_WIDTH), lambda b, pt: (b, 0, 0))
    wspec = lambda w: pl.BlockSpec(w.shape, lambda b, pt: (0,) * w.ndim)
    return pl.pallas_call(
        functools.partial(_compress_decode_kernel, n_pages=n_pages, n_valid=n_valid),
        grid_spec=pltpu.PrefetchScalarGridSpec(
            num_scalar_prefetch=1, grid=(nb,),
            in_specs=_page_specs((1, PAGE_SIZE // CMP_STRIDE, CHUNK_FLAT), n_pages) * 2
            + [wspec(w) for w in kweights + vweights],
            out_specs=[out, out]),
        out_shape=[jax.ShapeDtypeStruct((nb, rows, KV_WIDTH), BF16)] * 2,
        compiler_params=_cparams(("arbitrary",)),
        name="compress_decode",
    )(page_table, *([ck_pages] * n_pages), *([cv_pages] * n_pages), *kweights, *vweights)


def _place_heads(o_heads, grp):
    low_half = _iota(o_heads[0].shape, 1) < 64
    chunks = []
    for c in range(2):
        even, odd = o_heads[2 * c], o_heads[2 * c + 1]
        if grp == 0:
            odd = _swap_halves(odd)
        else:
            even = _swap_halves(even)
        chunks.append(jnp.where(low_half, even, odd))
    return chunks


def _gate_chunk(gates, grp, c, branch, low_half):
    base = (grp * NSA_REP + 2 * c) * N_BRANCH + branch
    return jnp.where(low_half, gates[:, base:base + 1], gates[:, base + N_BRANCH:base + N_BRANCH + 1])


def _rank_select(score, idx, n_iter):
    count = jnp.zeros(score.shape, F32)
    for j in range(n_iter):
        row = score[j:j + 1, :]
        ahead = (row > score) | ((row == score) & (idx > j))
        count = count + jnp.where(ahead, 1.0, 0.0)
    return count < float(SEL_TOPK)


NSA_TQ = 128
N_CMP_PAD = 256


def _cmp_topk_kernel(q_ref, kc_ref, vc_ref, cov_ref, eye_ref, ocmp_ref, sel_ref):
    tq = NSA_TQ
    q0 = pl.program_id(1) * tq
    kc = kc_ref[0]
    vc = vc_ref[0]
    cov = cov_ref[...]
    qpos_r = q0 + _iota((tq, N_CMP_PAD), 0)
    cmask = (_iota((tq, N_CMP_PAD), 1) * CMP_STRIDE + (CMP_LEN - 1)) <= qpos_r
    qpos_c = q0 + _iota((N_CMP_PAD, tq), 1)
    cmask_t = (_iota((N_CMP_PAD, tq), 0) * CMP_STRIDE + (CMP_LEN - 1)) <= qpos_c
    n_sel = SEL_LEN
    jj = _iota((n_sel, tq), 0)
    qp = q0 + _iota((n_sel, tq), 1)
    qblk = qp >> 6
    forced = (jj == 0) | (jj == qblk) | (jj == qblk - 1)
    valid = (jj << 6) <= qp
    sel_rows = []
    for grp in range(NSA_KV_HEADS):
        o_heads = []
        imp = jnp.zeros((n_sel, tq), F32)
        for r in range(NSA_REP):
            hd = grp * NSA_REP + r
            qh = q_ref[0, :, hd * LANES:(hd + 1) * LANES]
            s = jnp.where(cmask, _nt(qh, kc), NEG_BIG)
            e = jnp.where(cmask, jnp.exp(s - jnp.max(s, axis=-1, keepdims=True)), 0.0)
            p = e / jnp.maximum(jnp.sum(e, axis=-1, keepdims=True), 1e-30)
            o_heads.append(_nn(p.astype(BF16), vc))
            st = jnp.where(cmask_t, _nt(kc, qh), NEG_BIG)
            et = jnp.where(cmask_t, jnp.exp(st - jnp.max(st, axis=0, keepdims=True)), 0.0)
            pt = et / jnp.maximum(jnp.sum(et, axis=0, keepdims=True), 1e-30)
            imp = imp + _nn(cov, pt.astype(BF16))
        chunks = _place_heads(o_heads, grp)
        for c in range(2):
            col = (grp * 2 + c) * LANES
            ocmp_ref[0, :, col:col + LANES] = chunks[c]
        score = jnp.where(forced, FORCE_SCORE, jnp.where(valid, imp, -1.0))
        sel_rows.append(jnp.where(_rank_select(score, jj, n_sel), 1.0, 0.0))
    sel_t = jnp.concatenate(sel_rows, axis=0).astype(BF16)
    sel_ref[0] = _nt(eye_ref[...], sel_t).astype(BF16)


def _cover_t(n_sel, n_cmp_pad, n_cmp):
    n = jnp.arange(n_cmp_pad)
    j0 = jnp.arange(n_sel) * SEL_LEN
    c_start = n * CMP_STRIDE
    c_end = c_start + CMP_LEN - 1
    cover = (c_start[None, :] < j0[:, None] + SEL_LEN) & (c_end[None, :] >= j0[:, None]) & (n[None, :] < n_cmp)
    return cover.astype(BF16)


def _cmp_topk(qn, kc, vc):
    b, t, _ = qn.shape
    tq = NSA_TQ
    n_cmp = (t - CMP_LEN) // CMP_STRIDE + 1
    cov = _cover_t(SEL_LEN, N_CMP_PAD, n_cmp)
    eye = jnp.eye(tq, dtype=BF16)
    return pl.pallas_call(
        _cmp_topk_kernel,
        grid=(b, t // tq),
        in_specs=[pl.BlockSpec((1, tq, NSA_HEADS * LANES), lambda bi, qi: (bi, qi, 0)),
                  pl.BlockSpec((1, N_CMP_PAD, KV_WIDTH), lambda bi, qi: (bi, 0, 0)),
                  pl.BlockSpec((1, N_CMP_PAD, KV_WIDTH), lambda bi, qi: (bi, 0, 0)),
                  _const_spec(cov), _const_spec(eye)],
        out_specs=[pl.BlockSpec((1, tq, NSA_WIDTH), lambda bi, qi: (bi, qi, 0)),
                   pl.BlockSpec((1, tq, LANES), lambda bi, qi: (bi, qi, 0))],
        out_shape=[jax.ShapeDtypeStruct((b, t, NSA_WIDTH), F32), jax.ShapeDtypeStruct((b, t, LANES), BF16)],
        compiler_params=_cparams(("parallel", "arbitrary")),
        name="cmp_topk",
    )(qn, kc, vc, cov, eye)


SEL_TK = 256
WIN_TK = 128


def _online_step(s3, mask, v, carry):
    m, l, acc = carry
    heads, tq, tk = s3.shape
    s3 = jnp.where(mask[None], s3, NEG_BIG)
    m_new = jnp.maximum(m, jnp.max(s3, axis=-1, keepdims=True))
    alpha = jnp.exp(m - m_new)
    p = jnp.where(mask[None], jnp.exp(s3 - m_new), 0.0)
    l = alpha * l + jnp.sum(p, axis=-1, keepdims=True)
    pv = _nn(p.reshape(heads * tq, tk).astype(BF16), v).reshape(heads, tq, LANES)
    return m_new, l, alpha * acc + pv


def _nsa_prompt_kernel(q_ref, sk_ref, sv_ref, wk_ref, wv_ref, sel_ref, e_ref, ocmp_ref, gate_ref, o_ref):
    tq = NSA_TQ
    qi = pl.program_id(1)
    q0 = qi * tq
    gates = gate_ref[0]
    sel = sel_ref[0]
    low_half = _iota((tq, LANES), 1) < 64
    init = (jnp.full((NSA_REP, tq, 1), NEG_BIG, F32), jnp.zeros((NSA_REP, tq, 1), F32),
            jnp.zeros((NSA_REP, tq, LANES), F32))
    qpos_s = q0 + _iota((tq, SEL_TK), 0)
    qpos_w = q0 + _iota((tq, WIN_TK), 0)
    for grp in range(NSA_KV_HEADS):
        q4 = jnp.concatenate([q_ref[0, :, (grp * NSA_REP + r) * LANES:(grp * NSA_REP + r + 1) * LANES]
                              for r in range(NSA_REP)], axis=0)

        def sel_body(j, carry):
            start = pl.multiple_of(j * SEL_TK, SEL_TK)
            s3 = _nt(q4, sk_ref[0, pl.ds(start, SEL_TK), :]).reshape(NSA_REP, tq, SEL_TK)
            kpos = start + _iota((tq, SEL_TK), 1)
            mask = (_nn(sel, e_ref[grp, j]) > 0.5) & (kpos <= qpos_s)
            return _online_step(s3, mask, sv_ref[0, pl.ds(start, SEL_TK), :], carry)

        _, l_sel, acc_sel = lax.fori_loop(0, q0 // SEL_TK + 1, sel_body, init)

        def win_body(i, carry):
            start = pl.multiple_of((qi - i) * WIN_TK, WIN_TK)
            s3 = _nt(q4, wk_ref[0, pl.ds(start, WIN_TK), :]).reshape(NSA_REP, tq, WIN_TK)
            kpos = start + _iota((tq, WIN_TK), 1)
            mask = (kpos <= qpos_w) & (kpos >= qpos_w - WINDOW)
            return _online_step(s3, mask, wv_ref[0, pl.ds(start, WIN_TK), :], carry)

        _, l_win, acc_win = lax.fori_loop(0, jnp.minimum(qi, WINDOW // WIN_TK) + 1, win_body, init)

        o_sel = acc_sel / l_sel
        o_win = acc_win / l_win
        sel_chunks = _place_heads([o_sel[r] for r in range(NSA_REP)], grp)
        win_chunks = _place_heads([o_win[r] for r in range(NSA_REP)], grp)
        for c in range(2):
            col = (grp * 2 + c) * LANES
            o_ref[0, :, col:col + LANES] = (
                _gate_chunk(gates, grp, c, 0, low_half) * ocmp_ref[0, :, col:col + LANES]
                + _gate_chunk(gates, grp, c, 1, low_half) * sel_chunks[c]
                + _gate_chunk(gates, grp, c, 2, low_half) * win_chunks[c])


def _block_expand(t, tk):
    row = jnp.arange(NSA_KV_HEADS * SEL_LEN)
    key = jnp.arange(t)
    hit = (row[:, None] % SEL_LEN) == (key[None, :] // SEL_LEN)
    per_grp = jnp.stack([hit & ((row[:, None] // SEL_LEN) == g) for g in range(NSA_KV_HEADS)])
    return per_grp.reshape(NSA_KV_HEADS, NSA_KV_HEADS * SEL_LEN, t // tk, tk).transpose(0, 2, 1, 3).astype(BF16)


def _nsa_prompt(qn, skb, svb, wkb, wvb, sel, ocmp, gates):
    b, t, _ = qn.shape
    tq = NSA_TQ
    expand = _block_expand(t, SEL_TK)
    tile = lambda width: pl.BlockSpec((1, tq, width), lambda bi, qi: (bi, qi, 0))
    seq = pl.BlockSpec((1, t, KV_WIDTH), lambda bi, qi: (bi, 0, 0))
    return pl.pallas_call(
        _nsa_prompt_kernel,
        grid=(b, t // tq),
        in_specs=[tile(NSA_HEADS * LANES), seq, seq, seq, seq, tile(LANES), _const_spec(expand),
                  tile(NSA_WIDTH), tile(LANES)],
        out_specs=tile(NSA_WIDTH),
        out_shape=jax.ShapeDtypeStruct((b, t, NSA_WIDTH), F32),
        compiler_params=_cparams(("parallel", "arbitrary")),
        name="nsa_prompt",
    )(qn, skb, svb, wkb, wvb, sel, expand, ocmp, gates)


def _sb_decode_kernel(pt_ref, q_ref, *refs, n_pages):
    kp, vp = refs[:n_pages], refs[n_pages:2 * n_pages]
    u_ref, o_ref = refs[2 * n_pages:]
    q8 = q_ref[0]
    pair_of_row = _iota((SB_HEADS, LANES), 0) >> 1
    q_bd = jnp.concatenate([jnp.where(pair_of_row == c, q8, jnp.zeros_like(q8))
                            for c in range(SB_WIDTH // LANES)], axis=-1)
    own = (_iota((SB_HEADS, SB_WIDTH), 1) >> 6) == _iota((SB_HEADS, SB_WIDTH), 0)
    upper = u_ref[...]
    carry = jnp.zeros((SB_HEADS, 1), F32)
    acc = jnp.zeros((SB_HEADS, SB_WIDTH), F32)
    for p in reversed(range(n_pages)):
        s = _nt(q_bd, kp[p][0].astype(BF16))
        sp = _softplus(s)
        log_keep = -sp
        hi, lo = _split_bf16(log_keep)
        rest = _nn(hi, upper) + _nn(lo, upper) + carry
        a = jnp.exp((s - sp) + rest)
        acc = acc + _nn(a.astype(BF16), vp[p][0].astype(BF16))
        carry = carry + jnp.sum(log_keep, axis=-1, keepdims=True)
    o_ref[0] = jnp.sum(jnp.where(own, acc, 0.0), axis=0, keepdims=True)


def _sb_decode(page_table, q, k_pages, v_pages):
    nb, n_pages = page_table.shape
    upper = _strict_upper(PAGE_SIZE)
    row = pl.BlockSpec((1, 1, SB_WIDTH), lambda b, pt: (b, 0, 0))
    qspec = pl.BlockSpec((1, SB_HEADS, LANES), lambda b, pt: (b, 0, 0))
    return pl.pallas_call(
        functools.partial(_sb_decode_kernel, n_pages=n_pages),
        grid_spec=pltpu.PrefetchScalarGridSpec(
            num_scalar_prefetch=1, grid=(nb,),
            in_specs=[qspec] + _page_specs((1, PAGE_SIZE, SB_WIDTH), n_pages) * 2
            + [pl.BlockSpec(upper.shape, lambda b, pt: (0, 0))],
            out_specs=row),
        out_shape=jax.ShapeDtypeStruct((nb, 1, SB_WIDTH), F32),
        compiler_params=_cparams(("arbitrary",)),
        name="sb_decode",
    )(page_table, q, *([k_pages] * n_pages), *([v_pages] * n_pages), upper)


def _nsa_decode_kernel(pt_ref, q_ref, kc_ref, vc_ref, cov_ref, exp_ref, gate_ref,
                       sknew_ref, svnew_ref, wknew_ref, wvnew_ref, wks_ref, wvs_ref, *refs,
                       n_pages, n_cmp, n_sel):
    kp, vp = refs[:n_pages], refs[n_pages:2 * n_pages]
    o_ref, wko_ref, wvo_ref = refs[2 * n_pages:]
    heads = NSA_HEADS
    q8 = q_ref[0]
    q8f = q8.astype(F32)
    row8 = _iota((heads, LANES), 0)
    lane8 = _iota((heads, LANES), 1)
    grp_of_row = row8 >> 2

    cvalid = lane8 < n_cmp
    s = jnp.where(cvalid, _nt(q8, kc_ref[0]), NEG_BIG)
    e = jnp.where(cvalid, jnp.exp(s - jnp.max(s, axis=-1, keepdims=True)), 0.0)
    p = e / jnp.maximum(jnp.sum(e, axis=-1, keepdims=True), 1e-30)
    pb = p.astype(BF16)
    o_cmp = _nn(pb, vc_ref[0])
    imp8 = _nn(pb, cov_ref[...])

    jlane = _iota((LANES, LANES), 1)
    jrow = _iota((LANES, LANES), 0)
    q_blk = n_sel - 1
    sel_rows = []
    for grp in range(NSA_KV_HEADS):
        imp = jnp.sum(jnp.where(grp_of_row == grp, imp8, 0.0), axis=0, keepdims=True)
        impm = jnp.broadcast_to(imp, (LANES, LANES))
        forced = (jlane == 0) | (jlane == q_blk) | (jlane == q_blk - 1)
        score = jnp.where(jlane < n_sel, jnp.where(forced, FORCE_SCORE, impm), -2.0)
        score_t = score.T
        ahead = (score_t > score) | ((score_t == score) & (jrow < jlane))
        count = jnp.sum(jnp.where(ahead, 1.0, 0.0), axis=0, keepdims=True)
        sel_rows.append(jnp.where((count < float(SEL_TOPK)) & (jlane[:1] < n_sel), 1.0, 0.0))
    sel8 = jnp.where(grp_of_row == 0, sel_rows[0], sel_rows[1])
    new_sel = jnp.sum(jnp.where(lane8 == q_blk, sel8, 0.0), axis=-1, keepdims=True) > 0.5

    def new_token_score(knew_ref):
        kb = knew_ref[0].astype(BF16).astype(F32)
        return jnp.sum(q8f * kb, axis=-1, keepdims=True)

    page_keys = PAGE_SIZE
    s_pages = [_nt(q8, kp[pg][0].astype(BF16)) for pg in range(n_pages)]
    s_all = jnp.concatenate(s_pages, axis=-1)
    mask_all = _nn(sel8.astype(BF16), exp_ref[...]) > 0.5
    s_all = jnp.where(mask_all, s_all, NEG_BIG)
    s_new = jnp.where(new_sel, new_token_score(sknew_ref), NEG_BIG)
    m = jnp.maximum(jnp.max(s_all, axis=-1, keepdims=True), s_new)
    p_all = jnp.where(mask_all, jnp.exp(s_all - m), 0.0)
    p_new = jnp.where(new_sel, jnp.exp(s_new - m), 0.0)
    denom = jnp.sum(p_all, axis=-1, keepdims=True) + p_new
    acc = p_new.astype(BF16).astype(F32) * svnew_ref[0].astype(BF16).astype(F32)
    p_allb = p_all.astype(BF16)
    for pg in range(n_pages):
        acc = acc + _nn(p_allb[:, pg * page_keys:(pg + 1) * page_keys], vp[pg][0].astype(BF16))
    o_sel = acc / denom

    wk_state = wks_ref[0]
    wv_state = wvs_ref[0]
    s_w = _nt(q8, wk_state.astype(BF16))
    s_wn = new_token_score(wknew_ref)
    m = jnp.maximum(jnp.max(s_w, axis=-1, keepdims=True), s_wn)
    p_w = jnp.exp(s_w - m)
    p_wn = jnp.exp(s_wn - m)
    denom = jnp.sum(p_w, axis=-1, keepdims=True) + p_wn
    acc = _nn(p_w.astype(BF16), wv_state.astype(BF16))
    acc = acc + p_wn.astype(BF16).astype(F32) * wvnew_ref[0].astype(BF16).astype(F32)
    o_win = acc / denom

    gates = jnp.broadcast_to(gate_ref[0], (heads, LANES))

    def gate(branch):
        return jnp.sum(jnp.where(lane8 == row8 * N_BRANCH + branch, gates, 0.0), axis=-1, keepdims=True)

    o8 = gate(0) * o_cmp + gate(1) * o_sel + gate(2) * o_win
    low_half = _iota((1, LANES), 1) < 64
    chunks = []
    for c in range(NSA_HEADS // 2):
        grp = c // 2
        even, odd = o8[2 * c:2 * c + 1], o8[2 * c + 1:2 * c + 2]
        if grp == 0:
            odd = _swap_halves(odd)
        else:
            even = _swap_halves(even)
        chunks.append(jnp.where(low_half, even, odd))
    o_ref[0] = jnp.concatenate(chunks, axis=-1)

    keep = wk_state.shape[0]
    last = _iota((keep, LANES), 0) == keep - 1
    wko_ref[0] = jnp.where(last, wknew_ref[0], pltpu.roll(wk_state, keep - 1, 0))
    wvo_ref[0] = jnp.where(last, wvnew_ref[0], pltpu.roll(wv_state, keep - 1, 0))


def _nsa_decode(page_table, q8, kc, vc, gates, sk_new, sv_new, wk_new, wv_new, wk_state, wv_state,
                sk_pages, sv_pages):
    nb, n_pages = page_table.shape
    past = n_pages * PAGE_SIZE
    keep = wk_state.shape[1]
    n_cmp = (past + 1 - CMP_LEN) // CMP_STRIDE + 1
    n_sel = -(-(past + 1) // SEL_LEN)
    cov = _cover_t(LANES, LANES, n_cmp).T
    expand = ((jnp.arange(LANES)[:, None] == (jnp.arange(past)[None, :] // SEL_LEN))).astype(BF16)
    per_seq = lambda shape: pl.BlockSpec((1,) + shape, lambda b, pt: (b, 0, 0))
    const = lambda a: pl.BlockSpec(a.shape, lambda b, pt: (0,) * a.ndim)
    return pl.pallas_call(
        functools.partial(_nsa_decode_kernel, n_pages=n_pages, n_cmp=n_cmp, n_sel=n_sel),
        grid_spec=pltpu.PrefetchScalarGridSpec(
            num_scalar_prefetch=1, grid=(nb,),
            in_specs=[per_seq((NSA_HEADS, LANES)), per_seq((LANES, KV_WIDTH)), per_seq((LANES, KV_WIDTH)),
                      const(cov), const(expand), per_seq((1, LANES))]
            + [per_seq((1, KV_WIDTH))] * 4 + [per_seq((keep, KV_WIDTH))] * 2
            + _page_specs((1, PAGE_SIZE, KV_WIDTH), n_pages) * 2,
            out_specs=[per_seq((1, NSA_WIDTH)), per_seq((keep, KV_WIDTH)), per_seq((keep, KV_WIDTH))]),
        out_shape=[jax.ShapeDtypeStruct((nb, 1, NSA_WIDTH), F32),
                   jax.ShapeDtypeStruct((nb, keep, KV_WIDTH), F32),
                   jax.ShapeDtypeStruct((nb, keep, KV_WIDTH), F32)],
        compiler_params=_cparams(("arbitrary",)),
        name="nsa_decode",
    )(page_table, q8, kc, vc, cov, expand, gates, sk_new, sv_new, wk_new, wv_new, wk_state, wv_state,
      *([sk_pages] * n_pages), *([sv_pages] * n_pages))


def _finish_a_kernel(osb_ref, onsa_ref, x_ref, gsb_ref, gnsa_ref, wout_ref, gffn_ref, wr_ref, rb_ref,
                     x1_ref, h_ref, comb_ref):
    a = _rms(osb_ref[...], gsb_ref[...]).astype(BF16)
    b = _rms(onsa_ref[...], gnsa_ref[...]).astype(BF16)
    x1 = x_ref[...] + (_nn(a, wout_ref[:SB_WIDTH, :]) + _nn(b, wout_ref[SB_WIDTH:, :]))
    x1_ref[...] = x1
    h = _rms(x1, gffn_ref[...]).astype(BF16)
    h_ref[...] = h
    tm = h.shape[0]
    logits = _nt(wr_ref[...], h) + rb_ref[...]
    g_logit = [logits[i:i + 1] for i in range(N_GROUPS)]
    g_max = functools.reduce(jnp.maximum, g_logit)
    g_idx = jnp.full((1, tm), N_GROUPS - 1, jnp.int32)
    for i in reversed(range(N_GROUPS - 1)):
        g_idx = jnp.where(g_logit[i] == g_max, i, g_idx)
    g_w = 1.0 / functools.reduce(lambda u, v: u + v, [jnp.exp(gl - g_max) for gl in g_logit])
    e_logit = logits[8:8 + N_EXPERTS]
    erow = _iota((N_EXPERTS, tm), 0)
    in_group = (erow >> 2) == g_idx
    k_idx = (erow & (EXPERTS_PER_GROUP - 1)).astype(F32)
    e_max = jnp.max(jnp.where(in_group, e_logit, NEG_BIG), axis=0, keepdims=True)
    e_exp = jnp.where(in_group, jnp.exp(e_logit - e_max), 0.0)
    prob = e_exp / jnp.sum(e_exp, axis=0, keepdims=True)
    big = float(EXPERTS_PER_GROUP)
    p1 = jnp.max(jnp.where(in_group, prob, -1.0), axis=0, keepdims=True)
    i1 = jnp.min(jnp.where(in_group & (prob == p1), k_idx, big), axis=0, keepdims=True)
    rest = in_group & (k_idx != i1)
    p2 = jnp.max(jnp.where(rest, prob, -1.0), axis=0, keepdims=True)
    i2 = jnp.min(jnp.where(rest & (prob == p2), k_idx, big), axis=0, keepdims=True)
    tot = p1 + p2
    top_w = jnp.where(k_idx == i1, p1 / tot, jnp.where(k_idx == i2, p2 / tot, 0.0))
    comb = jnp.where(in_group, g_w * top_w, 0.0)
    comb = jnp.concatenate([comb, jnp.zeros((LANES - N_EXPERTS, tm), F32)], axis=0)
    comb_ref[...] = comb.T


def _finish_a(osb, onsa, x, gsb, gnsa, wout, gffn, wr, rb, tm):
    n = x.shape[0]
    row = lambda width: pl.BlockSpec((tm, width), lambda i: (i, 0))
    rbx = jnp.broadcast_to(rb, (ROUTER_ROWS, tm))
    consts = [gsb, gnsa, wout, gffn, wr, rbx]
    return pl.pallas_call(
        _finish_a_kernel,
        grid=(n // tm,),
        in_specs=[row(SB_WIDTH), row(NSA_WIDTH), row(D_MODEL)] + [_const_spec(c) for c in consts],
        out_specs=[row(D_MODEL), row(D_MODEL), row(LANES)],
        out_shape=[jax.ShapeDtypeStruct((n, D_MODEL), F32), jax.ShapeDtypeStruct((n, D_MODEL), BF16),
                   jax.ShapeDtypeStruct((n, LANES), F32)],
        compiler_params=_cparams(("parallel",)),
        name="finish_a",
    )(osb, onsa, x, *consts)


def _moe_kernel(h_ref, comb_ref, x1_ref, wg_ref, wu_ref, wd_ref, gfin_ref, y_ref, acc_ref):
    e = pl.program_id(1)

    @pl.when(e == 0)
    def _():
        acc_ref[...] = jnp.zeros_like(acc_ref)

    h = h_ref[...]
    gate = _nn(h, wg_ref[0])
    up = _nn(h, wu_ref[0])
    comb = comb_ref[...]
    c = jnp.sum(jnp.where(_iota(comb.shape, 1) == e, comb, 0.0), axis=-1, keepdims=True)
    act = (gate / (1.0 + jnp.exp(-gate))) * up * c
    acc_ref[...] += _nn(act.astype(BF16), wd_ref[0])

    @pl.when(e == pl.num_programs(1) - 1)
    def _():
        y_ref[...] = _rms(x1_ref[...] + acc_ref[...], gfin_ref[...])


def _moe(h, comb, x1, wg, wu, wd, gfin, tm):
    n = h.shape[0]
    row = lambda width: pl.BlockSpec((tm, width), lambda i, e: (i, 0))
    return pl.pallas_call(
        _moe_kernel,
        grid=(n // tm, N_EXPERTS),
        in_specs=[row(D_MODEL), row(LANES), row(D_MODEL),
                  pl.BlockSpec((1, D_MODEL, D_FF_EXPERT), lambda i, e: (e, 0, 0)),
                  pl.BlockSpec((1, D_MODEL, D_FF_EXPERT), lambda i, e: (e, 0, 0)),
                  pl.BlockSpec((1, D_FF_EXPERT, D_MODEL), lambda i, e: (e, 0, 0)),
                  pl.BlockSpec((1, D_MODEL), lambda i, e: (0, 0))],
        out_specs=row(D_MODEL),
        out_shape=jax.ShapeDtypeStruct((n, D_MODEL), F32),
        scratch_shapes=[pltpu.VMEM((tm, D_MODEL), F32)],
        compiler_params=_cparams(("parallel", "arbitrary")),
        name="moe",
    )(h, comb, x1, wg, wu, wd, gfin)


def _rope_tables(pos):
    half = HEAD_DIM // 2
    inv_freq = ROPE_THETA ** (-jnp.arange(half, dtype=F32) / half)
    ang = pos.astype(F32)[:, None] * inv_freq[None, :]
    cos, sin = jnp.cos(ang), jnp.sin(ang)
    return jnp.concatenate([cos] * 4, axis=-1), jnp.concatenate([-sin, sin, -sin, sin], axis=-1)


def _router_weights(group_w, group_b, expert_w, expert_b):
    wr = jnp.zeros((ROUTER_ROWS, D_MODEL), F32)
    wr = wr.at[:N_GROUPS].set(group_w.T)
    wr = wr.at[8:8 + N_EXPERTS].set(expert_w.reshape(D_MODEL, N_EXPERTS).T)
    rb = jnp.zeros((ROUTER_ROWS, 1), F32)
    rb = rb.at[:N_GROUPS, 0].set(group_b)
    rb = rb.at[8:8 + N_EXPERTS, 0].set(expert_b.reshape(N_EXPERTS))
    return wr.astype(BF16), rb


def kernel(x_prompt, x_sample, cache_sb_k, cache_sb_v, cache_cmp_k, cache_cmp_v, cache_sel_k, cache_sel_v,
           state_win_k, state_win_v, page_table, w_in, cmp_pe_k, cmp_pe_v, cmp_k_w1, cmp_k_w2,
           cmp_v_w1, cmp_v_w2, norm_sb_out, norm_nsa_out, w_out, norm_attn, norm_ffn,
           router_group_w, router_group_b, router_expert_w, router_expert_b, w_gate, w_up, w_down,
           norm_final):
    assert w_in.shape[0] == 1, "single layer"
    b, t, d = x_prompt.shape
    nb = x_sample.shape[0]
    assert x_sample.shape[1] == 1
    n_pool = cache_sb_k.shape[1]
    past = page_table.shape[1] * PAGE_SIZE
    keep = state_win_k.shape[2]

    w_in_p = jnp.pad(w_in[0], ((0, 0), (0, IN_COLS_PAD - w_in.shape[2]))).astype(BF16)
    g_attn = norm_attn[0].reshape(1, d)
    kw = _compress_weights(cmp_pe_k[0], cmp_k_w1[0], cmp_k_w2[0])
    vw = _compress_weights(cmp_pe_v[0], cmp_v_w1[0], cmp_v_w2[0])
    wr, rb = _router_weights(router_group_w[0], router_group_b[0], router_expert_w[0], router_expert_b[0])
    finish_consts = (norm_sb_out[0].reshape(1, SB_WIDTH), norm_nsa_out[0].reshape(1, NSA_WIDTH),
                     w_out[0].astype(BF16), norm_ffn[0].reshape(1, d), wr, rb)
    wg, wu, wd = w_gate[0].astype(BF16), w_up[0].astype(BF16), w_down[0].astype(BF16)
    g_fin = norm_final.reshape(1, d)

    def finish(osb, onsa, x, tm_a, tm_moe):
        x1, h, comb = _finish_a(osb, onsa, x, *finish_consts, tm_a)
        return _moe(h, comb, x1, wg, wu, wd, g_fin, tm_moe)

    tm = 256
    cos_p, sin_p = _rope_tables(jnp.arange(t, dtype=jnp.int32))
    xp = x_prompt.reshape(b * t, d)
    (sbk, sbv, ck, cv, sk, sv, wk, wv, gates, sbq2, sbkb, sbvb, qn, skb, svb, wkb, wvb) = _proj(
        xp, g_attn, w_in_p, cos_p, sin_p, tm, t // tm)
    seq = lambda a: a.reshape(b, t, a.shape[-1])
    o_sb = _sb_prompt(seq(sbq2), seq(sbkb), seq(sbvb))
    n_cmp = (t - CMP_LEN) // CMP_STRIDE + 1
    kc = _compress_prompt(ck.reshape(b, t // CMP_STRIDE, CHUNK_FLAT), kw, n_cmp)
    vc = _compress_prompt(cv.reshape(b, t // CMP_STRIDE, CHUNK_FLAT), vw, n_cmp)
    o_cmp, sel = _cmp_topk(seq(qn), kc, vc)
    o_nsa = _nsa_prompt(seq(qn), seq(skb), seq(svb), seq(wkb), seq(wvb), sel, o_cmp, seq(gates))
    y_prompt = finish(o_sb.reshape(b * t, SB_WIDTH), o_nsa.reshape(b * t, NSA_WIDTH), xp, 256, 512)

    cos_s, sin_s = _rope_tables(jnp.full((nb,), past, jnp.int32))
    xs = x_sample.reshape(nb, d)
    (sbk_s, sbv_s, ck_s, cv_s, sk_s, sv_s, wk_s, wv_s, gates_s, sbq2_s, _, _, qn_s, _, _, _, _) = _proj(
        xs, g_attn, w_in_p, cos_s, sin_s, nb, 1)
    pages = lambda c, width: c.reshape(n_pool, PAGE_SIZE, width)
    o_sb_s = _sb_decode(page_table, sbq2_s.reshape(nb, SB_HEADS, LANES), pages(cache_sb_k[0], SB_WIDTH), pages(cache_sb_v[0], SB_WIDTH))
    n_cmp_s = (past + 1 - CMP_LEN) // CMP_STRIDE + 1
    chunks = lambda c: c.reshape(n_pool, PAGE_SIZE // CMP_STRIDE, CHUNK_FLAT)
    kc_s, vc_s = _compress_decode(page_table, chunks(cache_cmp_k[0]), chunks(cache_cmp_v[0]), kw, vw, n_cmp_s)
    row3 = lambda a: a.reshape(nb, 1, a.shape[-1])
    o_nsa_s, win_k_s, win_v_s = _nsa_decode(
        page_table, qn_s.reshape(nb, NSA_HEADS, LANES), kc_s, vc_s, row3(gates_s),
        row3(sk_s), row3(sv_s), row3(wk_s), row3(wv_s),
        state_win_k[0].reshape(nb, keep, KV_WIDTH), state_win_v[0].reshape(nb, keep, KV_WIDTH),
        pages(cache_sel_k[0], KV_WIDTH), pages(cache_sel_v[0], KV_WIDTH))
    y_sample = finish(o_sb_s.reshape(nb, SB_WIDTH), o_nsa_s.reshape(nb, NSA_WIDTH), xs, nb, nb)

    heads = lambda a, n, h: a.reshape(1, n, -1, h, HEAD_DIM)
    p8 = lambda a: a.reshape(1, b, t, SB_HEADS, HEAD_DIM)
    p2 = lambda a: a.reshape(1, b, t, NSA_KV_HEADS, HEAD_DIM)
    s8 = lambda a: a.reshape(1, nb, 1, SB_HEADS, HEAD_DIM)
    s2 = lambda a: a.reshape(1, nb, 1, NSA_KV_HEADS, HEAD_DIM)
    keep_p = min(WINDOW, t)
    return (y_prompt.reshape(b, t, d), y_sample.reshape(nb, 1, d),
            p8(sbk), p8(sbv), p2(ck), p2(cv), p2(sk), p2(sv),
            p2(wk)[:, :, t - keep_p:], p2(wv)[:, :, t - keep_p:],
            s8(sbk_s), s8(sbv_s), s2(ck_s), s2(cv_s), s2(sk_s), s2(sv_s),
            win_k_s.reshape(1, nb, keep, NSA_KV_HEADS, HEAD_DIM),
            win_v_s.reshape(1, nb, keep, NSA_KV_HEADS, HEAD_DIM))
```
